```python
import math
import jax, jax.numpy as jnp
from jax import lax
import numpy as np

D_MODEL = 4096
BATCH = 4
SEQ = 2048
DEPTH = 2
DEC_BATCH = 8
DEC_SEQ = 4
PAST_LEN = 16384
PAGE_SIZE = 128

N_EVEN = (DEPTH + 1) // 2
N_ODD = DEPTH // 2
D_CONV = D_MODEL // 2
D_LRU = D_MODEL // 2
CONV_A_WIDTH = 3
LRU_CONV_WIDTH = 4
LRU_BLOCKS = 16
LRU_BLOCK = D_LRU // LRU_BLOCKS
LRU_C = 8.0
N_HEADS = 32
HEAD_DIM = D_MODEL // N_HEADS
N_KV_HEADS = 8
GROUP = N_HEADS // N_KV_HEADS
IDX_HEADS = 64
IDX_DIM = 128
TOPK_MAX = 256
N_BUCKETS = 32
MAX_DISTANCE = 128
Q_BLOCK = 128
Q_DIM = N_HEADS * HEAD_DIM
KV_DIM = N_KV_HEADS * HEAD_DIM
IDX_Q_DIM = IDX_HEADS * IDX_DIM
IDX_W_SCALE = (IDX_HEADS * IDX_DIM) ** -0.5
D_FF = 256 * ((8 * D_MODEL // 3 + 255) // 256)
N_EXPERTS = 8
TOP_K = 2
D_FF_EXPERT = 7 * D_MODEL // 2
EPS = 1e-6

kernel_name = 'hybrid_shortconv_rglru_dsa_moe_step'


def rms_norm(x, g):
    xf = x.astype(jnp.float32)
    y = xf * lax.rsqrt(jnp.mean(xf * xf, axis=-1, keepdims=True) + EPS)
    return (y * g.astype(jnp.float32)).astype(x.dtype)


def causal_depthwise_conv(u, buf, w):
    width = w.shape[0]
    t_len = u.shape[1]
    ext = jnp.concatenate([buf.astype(u.dtype), u], axis=1)
    y = ext[:, 0:t_len] * w[0]
    for k in range(1, width):
        y = y + ext[:, k:k + t_len] * w[k]
    return y, ext[:, t_len:]


def short_conv_mixer(v, c, b, buf, w_conv):
    y, new_buf = causal_depthwise_conv(c * v, buf, w_conv)
    return b * y, new_buf


def rglru_mixer(xb, yb, conv_buf, h0, w_conv, b_conv, w_r, b_r, w_i, b_i, lam):
    gate = jax.nn.gelu(yb)
    xc, new_buf = causal_depthwise_conv(xb, conv_buf, w_conv)
    xc = xc + b_conv
    nb, t_len, _ = xc.shape
    xh = xc.reshape(nb, t_len, LRU_BLOCKS, LRU_BLOCK)
    r = jax.nn.sigmoid(jnp.einsum('bthi,hij->bthj', xh, w_r).reshape(nb, t_len, D_LRU) + b_r)
    i = jax.nn.sigmoid(jnp.einsum('bthi,hij->bthj', xh, w_i).reshape(nb, t_len, D_LRU) + b_i)
    log_a = -LRU_C * r.astype(jnp.float32) * jax.nn.softplus(-lam.astype(jnp.float32))
    a = jnp.exp(log_a)
    bx = jnp.sqrt(-jnp.expm1(2.0 * log_a)) * (i * xc).astype(jnp.float32)

    def step(h, ab):
        a_t, b_t = ab
        h = a_t * h + b_t
        return h, h

    h_last, hs = lax.scan(step, h0.astype(jnp.float32), (a.swapaxes(0, 1), bx.swapaxes(0, 1)))
    h_seq = hs.swapaxes(0, 1).astype(xb.dtype)
    return h_seq * gate, new_buf, h_last.astype(h0.dtype)


def swiglu(h, w_gate, w_up, w_down):
    return (jax.nn.silu(h @ w_gate) * (h @ w_up)) @ w_down


def moe_swiglu(h, w_router, w_g, w_u, w_d):
    nb, t_len, dm = h.shape
    hf = h.reshape(-1, dm)
    logits = (hf @ w_router).astype(jnp.float32)
    top_v, top_i = lax.top_k(logits, TOP_K)
    gates = jax.nn.softmax(top_v, axis=-1)
    combine = jnp.sum(jax.nn.one_hot(top_i, N_EXPERTS, dtype=jnp.float32) * gates[..., None], axis=1)
    out = jnp.zeros(hf.shape, jnp.float32)
    for e in range(N_EXPERTS):
        ye = swiglu(hf, w_g[e], w_u[e], w_d[e])
        out = out + combine[:, e:e + 1] * ye.astype(jnp.float32)
    return out.astype(h.dtype).reshape(nb, t_len, dm)


def t5_bucket(dist):
    max_exact = N_BUCKETS // 2
    d = dist.astype(jnp.float32)
    large = max_exact + (jnp.log(jnp.maximum(d, 1.0) / max_exact) / math.log(MAX_DISTANCE / max_exact)
                         * (N_BUCKETS - max_exact)).astype(jnp.int32)
    large = jnp.minimum(large, N_BUCKETS - 1)
    return jnp.where(dist < max_exact, dist, large)


def indexer_scores(iq, iw, ik):
    s = jax.nn.relu(jnp.einsum('bthd,bsd->bths', iq, ik).astype(jnp.float32))
    return jnp.einsum('bths,bth->bts', s, iw.astype(jnp.float32))


def sparse_attend(q, q_pos, k_sel, v_sel, s_pos, valid, rel_bias):
    nb, t_len = q.shape[:2]
    n_sel = k_sel.shape[2]
    qg = q.reshape(nb, t_len, N_KV_HEADS, GROUP, HEAD_DIM)
    logits = jnp.einsum('btngd,btjnd->btngj', qg, k_sel).astype(jnp.float32) * HEAD_DIM ** -0.5
    bucket = t5_bucket(jnp.maximum(q_pos[None, :, None] - s_pos, 0))
    bias = rel_bias[bucket].astype(jnp.float32)
    bias = bias.reshape(nb, t_len, n_sel, N_KV_HEADS, GROUP).transpose(0, 1, 3, 4, 2)
    logits = jnp.where(valid[:, :, None, None, :], logits + bias, -jnp.inf)
    p = jax.nn.softmax(logits, axis=-1)
    out = jnp.einsum('btngj,btjnd->btngd', p.astype(v_sel.dtype), v_sel)
    return out.reshape(nb, t_len, Q_DIM)


def gather_rows(rows, idx):
    return jax.vmap(lambda r, s: r[s])(rows, idx)


def attn_prompt(q, k, v, iq, ik, iw, rel_bias):
    nb, s_len = q.shape[:2]
    n_sel = min(TOPK_MAX, s_len // 4)
    key_pos = jnp.arange(s_len, dtype=jnp.int32)

    def block(i):
        t0 = i * Q_BLOCK
        q_b = lax.dynamic_slice_in_dim(q, t0, Q_BLOCK, axis=1)
        iq_b = lax.dynamic_slice_in_dim(iq, t0, Q_BLOCK, axis=1)
        iw_b = lax.dynamic_slice_in_dim(iw, t0, Q_BLOCK, axis=1)
        q_pos = t0 + jnp.arange(Q_BLOCK, dtype=jnp.int32)
        sc = indexer_scores(iq_b, iw_b, ik)
        sc = jnp.where((key_pos[None, :] <= q_pos[:, None])[None], sc, -jnp.inf)
        _, sel = lax.top_k(sc, n_sel)
        valid = sel <= q_pos[None, :, None]
        return sparse_attend(q_b, q_pos, gather_rows(k, sel), gather_rows(v, sel), sel, valid, rel_bias)

    out = lax.map(block, jnp.arange(s_len // Q_BLOCK))
    return out.transpose(1, 0, 2, 3).reshape(nb, s_len, Q_DIM)


def attn_sample(q, k_new, v_new, iq, ik_new, iw, cache_k, cache_v, cache_ik, page_table, layer_o, rel_bias):
    nb, t_len = q.shape[:2]
    past = page_table.shape[1] * PAGE_SIZE
    total = past + t_len
    n_sel = min(TOPK_MAX, total // 4)
    ik_past = cache_ik[layer_o, page_table].reshape(nb, past, IDX_DIM)
    ik_all = jnp.concatenate([ik_past.astype(ik_new.dtype), ik_new], axis=1)
    q_pos = past + jnp.arange(t_len, dtype=jnp.int32)
    key_pos = jnp.arange(total, dtype=jnp.int32)
    sc = indexer_scores(iq, iw, ik_all)
    sc = jnp.where((key_pos[None, :] <= q_pos[:, None])[None], sc, -jnp.inf)
    _, sel = lax.top_k(sc, n_sel)
    valid = sel <= q_pos[None, :, None]
    in_past = sel < past
    sp = jnp.minimum(sel, past - 1)
    phys = jnp.take_along_axis(page_table, (sp // PAGE_SIZE).reshape(nb, -1), axis=1).reshape(sel.shape)
    off = sp % PAGE_SIZE
    sn = jnp.clip(sel - past, 0, t_len - 1)
    mask = in_past[..., None, None]
    k_sel = jnp.where(mask, cache_k[layer_o, phys, off].astype(k_new.dtype), gather_rows(k_new, sn))
    v_sel = jnp.where(mask, cache_v[layer_o, phys, off].astype(v_new.dtype), gather_rows(v_new, sn))
    return sparse_attend(q, q_pos, k_sel, v_sel, sel, valid, rel_bias)


def run_trunk(x, conv_a_buf, lru_buf, lru_h, paged, p):
    nb, t_len, _ = x.shape
    new_conv, new_lbuf, new_h, new_k, new_v, new_ik = [], [], [], [], [], []
    splits_ab = [D_CONV, 2 * D_CONV, 3 * D_CONV, 3 * D_CONV + D_LRU]
    splits_c = [Q_DIM, Q_DIM + KV_DIM, Q_DIM + 2 * KV_DIM, Q_DIM + 2 * KV_DIM + IDX_Q_DIM,
                Q_DIM + 2 * KV_DIM + IDX_Q_DIM + IDX_DIM]
    for layer in range(DEPTH):
        li = layer // 2
        h = rms_norm(x, p['g_mix'][layer])
        if layer % 2 == 0:
            proj = h @ p['w_in_ab'][li]
            v_a, c_a, b_a, x_l, y_l = jnp.split(proj, splits_ab, axis=-1)
            out_a, nb_a = short_conv_mixer(v_a, c_a, b_a, conv_a_buf[li], p['w_conv_a'][li])
            out_l, nb_l, nh = rglru_mixer(x_l, y_l, lru_buf[li], lru_h[li], p['w_conv_lru'][li],
                                          p['b_conv_lru'][li], p['w_gate_r'][li], p['b_gate_r'][li],
                                          p['w_gate_i'][li], p['b_gate_i'][li], p['lru_lambda'][li])
            x = x + jnp.concatenate([out_a, out_l], axis=-1) @ p['w_out_ab'][li]
            x = x + swiglu(rms_norm(x, p['g_ffn'][layer]), p['w_ffn_gate'][li], p['w_ffn_up'][li],
                           p['w_ffn_down'][li])
            new_conv.append(nb_a)
            new_lbuf.append(nb_l)
            new_h.append(nh)
        else:
            proj = h @ p['w_in_attn'][li]
            q, k, v, iq, ik, iw = jnp.split(proj, splits_c, axis=-1)
            q = rms_norm(q.reshape(nb, t_len, N_HEADS, HEAD_DIM), p['q_norm'][li])
            k = rms_norm(k.reshape(nb, t_len, N_KV_HEADS, HEAD_DIM), p['k_norm'][li])
            v = v.reshape(nb, t_len, N_KV_HEADS, HEAD_DIM)
            iq = iq.reshape(nb, t_len, IDX_HEADS, IDX_DIM)
            iw = iw * IDX_W_SCALE
            if paged is None:
                att = attn_prompt(q, k, v, iq, ik, iw, p['rel_bias'])
            else:
                cache_k, cache_v, cache_ik, page_table = paged
                att = attn_sample(q, k, v, iq, ik, iw, cache_k, cache_v, cache_ik, page_table, li, p['rel_bias'])
            x = x + att @ p['w_out_attn'][li]
            x = x + moe_swiglu(rms_norm(x, p['g_ffn'][layer]), p['w_router'][li], p['w_exp_gate'][li],
                               p['w_exp_up'][li], p['w_exp_down'][li])
            new_k.append(k)
            new_v.append(v)
            new_ik.append(ik)
    return x, (jnp.stack(new_conv), jnp.stack(new_lbuf), jnp.stack(new_h),
               jnp.stack(new_k), jnp.stack(new_v), jnp.stack(new_ik))


def setup_inputs(seed: int = 0) -> dict:
    key = jax.random.key(seed)
    ks = iter(jax.random.split(key, 48))
    n_pages = PAST_LEN // PAGE_SIZE
    n_used = DEC_BATCH * n_pages
    n_pool = n_used + max(1, n_used // 4)

    def nrm(shape, scale):
        return jax.random.normal(next(ks), shape, jnp.float32) * scale

    d_in_ab = 3 * D_CONV + 2 * D_LRU
    d_in_attn = Q_DIM + 2 * KV_DIM + IDX_Q_DIM + IDX_DIM + IDX_HEADS
    u = jax.random.uniform(next(ks), (N_EVEN, D_LRU), jnp.float32, 0.9, 0.999)
    s = u ** (1.0 / LRU_C)
    lru_lambda = jnp.log(s) - jnp.log1p(-s)
    page_table = jax.random.permutation(next(ks), n_pool)[:n_used].reshape(DEC_BATCH, n_pages).astype(jnp.int32)
    inv = D_MODEL ** -0.5
    return {
        'x_prompt': nrm((BATCH, SEQ, D_MODEL), 1.0),
        'x_sample': nrm((DEC_BATCH, DEC_SEQ, D_MODEL), 1.0),
        'state_conv_a': nrm((N_EVEN, DEC_BATCH, CONV_A_WIDTH - 1, D_CONV), 1.0),
        'state_lru_conv': nrm((N_EVEN, DEC_BATCH, LRU_CONV_WIDTH - 1, D_LRU), 1.0),
        'state_lru_h': nrm((N_EVEN, DEC_BATCH, D_LRU), 0.5),
        'cache_k': nrm((N_ODD, n_pool, PAGE_SIZE, N_KV_HEADS, HEAD_DIM), 1.0),
        'cache_v': nrm((N_ODD, n_pool, PAGE_SIZE, N_KV_HEADS, HEAD_DIM), 1.0),
        'cache_idx_k': nrm((N_ODD, n_pool, PAGE_SIZE, IDX_DIM), 1.0),
        'page_table': page_table,
        'g_mix': 1.0 + nrm((DEPTH, D_MODEL), 0.02),
        'g_ffn': 1.0 + nrm((DEPTH, D_MODEL), 0.02),
        'w_in_ab': nrm((N_EVEN, D_MODEL, d_in_ab), inv),
        'w_conv_a': nrm((N_EVEN, CONV_A_WIDTH, D_CONV), CONV_A_WIDTH ** -0.5),
        'w_conv_lru': nrm((N_EVEN, LRU_CONV_WIDTH, D_LRU), LRU_CONV_WIDTH ** -0.5),
        'b_conv_lru': nrm((N_EVEN, D_LRU), 0.01),
        'w_gate_r': nrm((N_EVEN, LRU_BLOCKS, LRU_BLOCK, LRU_BLOCK), LRU_BLOCK ** -0.5),
        'b_gate_r': nrm((N_EVEN, D_LRU), 0.01),
        'w_gate_i': nrm((N_EVEN, LRU_BLOCKS, LRU_BLOCK, LRU_BLOCK), LRU_BLOCK ** -0.5),
        'b_gate_i': nrm((N_EVEN, D_LRU), 0.01),
        'lru_lambda': lru_lambda,
        'w_out_ab': nrm((N_EVEN, D_CONV + D_LRU, D_MODEL), (D_CONV + D_LRU) ** -0.5),
        'w_ffn_gate': nrm((N_EVEN, D_MODEL, D_FF), inv),
        'w_ffn_up': nrm((N_EVEN, D_MODEL, D_FF), inv),
        'w_ffn_down': nrm((N_EVEN, D_FF, D_MODEL), D_FF ** -0.5),
        'w_in_attn': nrm((N_ODD, D_MODEL, d_in_attn), inv),
        'q_norm': 1.0 + nrm((N_ODD, HEAD_DIM), 0.02),
        'k_norm': 1.0 + nrm((N_ODD, HEAD_DIM), 0.02),
        'rel_bias': nrm((N_BUCKETS, N_HEADS), 0.1),
        'w_out_attn': nrm((N_ODD, Q_DIM, D_MODEL), Q_DIM ** -0.5),
        'w_router': nrm((N_ODD, D_MODEL, N_EXPERTS), inv),
        'w_exp_gate': nrm((N_ODD, N_EXPERTS, D_MODEL, D_FF_EXPERT), inv),
        'w_exp_up': nrm((N_ODD, N_EXPERTS, D_MODEL, D_FF_EXPERT), inv),
        'w_exp_down': nrm((N_ODD, N_EXPERTS, D_FF_EXPERT, D_MODEL), D_FF_EXPERT ** -0.5),
    }


def reference(x_prompt, x_sample, state_conv_a, state_lru_conv, state_lru_h, cache_k, cache_v, cache_idx_k,
              page_table, g_mix, g_ffn, w_in_ab, w_conv_a, w_conv_lru, b_conv_lru, w_gate_r, b_gate_r,
              w_gate_i, b_gate_i, lru_lambda, w_out_ab, w_ffn_gate, w_ffn_up, w_ffn_down, w_in_attn, q_norm,
              k_norm, rel_bias, w_out_attn, w_router, w_exp_gate, w_exp_up, w_exp_down):
    p = dict(g_mix=g_mix, g_ffn=g_ffn, w_in_ab=w_in_ab, w_conv_a=w_conv_a, w_conv_lru=w_conv_lru,
             b_conv_lru=b_conv_lru, w_gate_r=w_gate_r, b_gate_r=b_gate_r, w_gate_i=w_gate_i, b_gate_i=b_gate_i,
             lru_lambda=lru_lambda, w_out_ab=w_out_ab, w_ffn_gate=w_ffn_gate, w_ffn_up=w_ffn_up,
             w_ffn_down=w_ffn_down, w_in_attn=w_in_attn, q_norm=q_norm, k_norm=k_norm, rel_bias=rel_bias,
             w_out_attn=w_out_attn, w_router=w_router, w_exp_gate=w_exp_gate, w_exp_up=w_exp_up,
             w_exp_down=w_exp_down)
    nbp = x_prompt.shape[0]
    conv0 = jnp.zeros((N_EVEN, nbp, CONV_A_WIDTH - 1, D_CONV), x_prompt.dtype)
    lbuf0 = jnp.zeros((N_EVEN, nbp, LRU_CONV_WIDTH - 1, D_LRU), x_prompt.dtype)
    h0 = jnp.zeros((N_EVEN, nbp, D_LRU), x_prompt.dtype)
    y_prompt, st_p = run_trunk(x_prompt, conv0, lbuf0, h0, None, p)
    y_sample, st_s = run_trunk(x_sample, state_conv_a, state_lru_conv, state_lru_h,
                               (cache_k, cache_v, cache_idx_k, page_table), p)
    p_conv, p_lbuf, p_h, p_k, p_v, p_ik = st_p
    s_conv, s_lbuf, s_h, s_k, s_v, s_ik = st_s
    return (y_prompt, y_sample, p_conv, p_lbuf, p_h, p_k, p_v, p_ik, s_conv, s_lbuf, s_h, s_k, s_v, s_ik)
```

```python
import functools
import math

import numpy as np
import jax
import jax.numpy as jnp
from jax import lax
from jax.experimental import pallas as pl
from jax.experimental.pallas import tpu as pltpu

F32 = jnp.float32
BF16 = jnp.bfloat16

LANE = 128
VMEM_LIMIT_BYTES = 56 * 1024 * 1024
EPS = 1e-6
LRU_C = 8.0
N_BUCKETS = 32
MAX_DISTANCE = 128
MASK_NEG = -1e30
ATT_TILE = 256
IDX_TQ = 128
MOE_TILE = 256
MOE_DOWN_TILE = 512


class _Cfg:
    def __init__(self, d_model, batch, seq, dec_batch, dec_seq, past_len, n_heads, n_kv_heads, idx_heads,
                 topk_max, d_ff, n_experts, d_ff_expert):
        self.d_model = d_model
        self.batch = batch
        self.seq = seq
        self.dec_batch = dec_batch
        self.dec_seq = dec_seq
        self.past_len = past_len
        self.page = LANE
        self.n_pages = past_len // LANE
        self.n_heads = n_heads
        self.n_kv = n_kv_heads
        self.group = n_heads // n_kv_heads
        self.idx_heads = idx_heads
        self.topk_max = topk_max
        self.d_ff = d_ff
        self.n_experts = n_experts
        self.d_ff_expert = d_ff_expert
        self.d_half = d_model // 2
        self.q_dim = n_heads * LANE
        self.kv_dim = n_kv_heads * LANE
        self.idx_q_dim = idx_heads * LANE
        assert d_model // n_heads == LANE and self.group == 4


_CFG = _Cfg(d_model=4096, batch=4, seq=2048, dec_batch=8, dec_seq=4, past_len=16384, n_heads=32, n_kv_heads=8,
            idx_heads=64, topk_max=256, d_ff=11008, n_experts=8, d_ff_expert=14336)


def _params(sem):
    return pltpu.CompilerParams(dimension_semantics=sem, vmem_limit_bytes=VMEM_LIMIT_BYTES)


def _tile(n, pref, mult=LANE):
    if n <= pref:
        return n
    t = (pref // mult) * mult
    while t >= mult:
        if n % t == 0:
            return t
        t -= mult
    return n


def _rmsnorm_kernel(x_ref, g_ref, o_ref):
    x = x_ref[...]
    ms = jnp.mean(x * x, axis=-1, keepdims=True)
    o_ref[...] = ((x * lax.rsqrt(ms + EPS)) * g_ref[...]).astype(o_ref.dtype)


def _rmsnorm(x, g, out_dtype):
    m, d = x.shape
    tm = _tile(m, 256, 8)
    return pl.pallas_call(
        _rmsnorm_kernel,
        grid=(m // tm,),
        in_specs=[pl.BlockSpec((tm, d), lambda i: (i, 0)), pl.BlockSpec((1, d), lambda i: (0, 0))],
        out_specs=pl.BlockSpec((tm, d), lambda i: (i, 0)),
        out_shape=jax.ShapeDtypeStruct((m, d), out_dtype),
        compiler_params=_params(("parallel",)),
        name="rmsnorm",
    )(x, g.reshape(1, d))


def _mm_kernel(*refs, k_sizes, has_res, has_norm, n_out):
    n_x = len(k_sizes)
    x_refs = refs[:n_x]
    w_ref = refs[n_x]
    pos = n_x + 1
    res_ref = g_ref = None
    if has_res:
        res_ref = refs[pos]
        pos += 1
    if has_norm:
        g_ref = refs[pos]
        pos += 1
    out_refs = refs[pos:pos + n_out]
    acc = None
    off = 0
    for x_ref, ks in zip(x_refs, k_sizes):
        d = jnp.dot(x_ref[...], w_ref[off:off + ks, :], preferred_element_type=F32)
        acc = d if acc is None else acc + d
        off += ks
    if has_res:
        acc = res_ref[...] + acc
    if has_norm:
        g = g_ref[...]
        for c in range(acc.shape[1] // LANE):
            blk = acc[:, c * LANE:(c + 1) * LANE]
            ms = jnp.mean(blk * blk, axis=-1, keepdims=True)
            y = (blk * lax.rsqrt(ms + EPS)) * g
            for o_ref in out_refs:
                o_ref[:, c * LANE:(c + 1) * LANE] = y.astype(o_ref.dtype)
    else:
        for o_ref in out_refs:
            o_ref[...] = acc.astype(o_ref.dtype)


def _mm(xs, w, *, col_start=0, n_cols=None, res=None, norm_g=None, out_dtypes=(F32,), tm_pref=512, tn_pref=512):
    m = xs[0].shape[0]
    k_sizes = tuple(int(x.shape[1]) for x in xs)
    k_tot = sum(k_sizes)
    assert w.shape[0] == k_tot
    if n_cols is None:
        n_cols = w.shape[1] - col_start
    tm = _tile(m, tm_pref, 8)
    tn = _tile(n_cols, tn_pref)
    if tn % LANE != 0:
        assert col_start == 0 and n_cols == w.shape[1]
        col_off = 0
    else:
        assert col_start % tn == 0
        col_off = col_start // tn
    in_specs = [pl.BlockSpec((tm, ks), lambda i, j: (i, 0)) for ks in k_sizes]
    in_specs.append(pl.BlockSpec((k_tot, tn), lambda i, j: (0, j + col_off)))
    args = list(xs) + [w]
    if res is not None:
        in_specs.append(pl.BlockSpec((tm, tn), lambda i, j: (i, j)))
        args.append(res)
    if norm_g is not None:
        in_specs.append(pl.BlockSpec((1, LANE), lambda i, j: (0, 0)))
        args.append(norm_g.reshape(1, LANE).astype(F32))
    outs = pl.pallas_call(
        functools.partial(_mm_kernel, k_sizes=k_sizes, has_res=res is not None, has_norm=norm_g is not None,
                          n_out=len(out_dtypes)),
        grid=(m // tm, n_cols // tn),
        in_specs=in_specs,
        out_specs=[pl.BlockSpec((tm, tn), lambda i, j: (i, j)) for _ in out_dtypes],
        out_shape=[jax.ShapeDtypeStruct((m, n_cols), dt) for dt in out_dtypes],
        compiler_params=_params(("parallel", "arbitrary")),
        name="dense_matmul",
    )(*args)
    return tuple(outs)


def _ffn_gu_kernel(x_ref, wg_ref, wu_ref, o_ref):
    x = x_ref[...]
    g = jnp.dot(x, wg_ref[...], preferred_element_type=F32)
    u = jnp.dot(x, wu_ref[...], preferred_element_type=F32)
    o_ref[...] = ((g * jax.nn.sigmoid(g)) * u).astype(o_ref.dtype)


def _ffn_gate_up(x, wg, wu):
    m, k = x.shape
    n = wg.shape[1]
    tm = _tile(m, 1024, 8)
    tn = _tile(n, 256)
    return pl.pallas_call(
        _ffn_gu_kernel,
        grid=(m // tm, n // tn),
        in_specs=[pl.BlockSpec((tm, k), lambda i, j: (i, 0)),
                  pl.BlockSpec((k, tn), lambda i, j: (0, j)),
                  pl.BlockSpec((k, tn), lambda i, j: (0, j))],
        out_specs=pl.BlockSpec((tm, tn), lambda i, j: (i, j)),
        out_shape=jax.ShapeDtypeStruct((m, n), BF16),
        compiler_params=_params(("parallel", "arbitrary")),
        name="ffn_gate_up",
    )(x, wg, wu)


_PAD = 8


def _short_conv_kernel(v_ref, c_ref, b_ref, st_ref, w_ref, o_ref, nst_ref, ext_ref, *, t_len, t_valid):
    u = c_ref[0] * v_ref[0]
    ext_ref[0:_PAD, :] = jnp.zeros((_PAD, u.shape[1]), F32)
    ext_ref[_PAD - 2:_PAD, :] = st_ref[0]
    ext_ref[_PAD:_PAD + t_len, :] = u
    w = w_ref[...]
    y = ext_ref[_PAD - 2:_PAD - 2 + t_len, :] * w[0:1, :]
    y = y + ext_ref[_PAD - 1:_PAD - 1 + t_len, :] * w[1:2, :]
    y = y + u * w[2:3, :]
    o_ref[0] = (b_ref[0] * y).astype(o_ref.dtype)
    nst_ref[0] = ext_ref[_PAD + t_valid - 2:_PAD + t_valid, :]


def _short_conv(proj, state, w_conv, *, nb, t_len, t_valid, d_half):
    tc = _tile(d_half, 256)
    nc = d_half // tc
    return pl.pallas_call(
        functools.partial(_short_conv_kernel, t_len=t_len, t_valid=t_valid),
        grid=(nb, nc),
        in_specs=[pl.BlockSpec((1, t_len, tc), lambda b, j: (b, 0, j)),
                  pl.BlockSpec((1, t_len, tc), lambda b, j: (b, 0, nc + j)),
                  pl.BlockSpec((1, t_len, tc), lambda b, j: (b, 0, 2 * nc + j)),
                  pl.BlockSpec((1, 2, tc), lambda b, j: (b, 0, j)),
                  pl.BlockSpec((3, tc), lambda b, j: (0, j))],
        out_specs=[pl.BlockSpec((1, t_len, tc), lambda b, j: (b, 0, j)),
                   pl.BlockSpec((1, 2, tc), lambda b, j: (b, 0, j))],
        out_shape=[jax.ShapeDtypeStruct((nb, t_len, d_half), BF16),
                   jax.ShapeDtypeStruct((nb, 2, d_half), F32)],
        scratch_shapes=[pltpu.VMEM((_PAD + t_len, tc), F32)],
        compiler_params=_params(("parallel", "parallel")),
        name="short_conv_mixer",
    )(proj, proj, proj, state, w_conv)


def _softplus(x):
    return jnp.maximum(x, 0.0) + jnp.log1p(jnp.exp(-jnp.abs(x)))


def _rglru_kernel(x_ref, y_ref, st_ref, h0_ref, wc_ref, bc_ref, wr_ref, br_ref, wi_ref, bi_ref, lam_ref,
                  o_ref, nst_ref, nh_ref, ext_ref, a_ref, bx_ref, *, t_len, t_valid):
    tc = x_ref.shape[2]
    xl = x_ref[0]
    ext_ref[0:_PAD, :] = jnp.zeros((_PAD, tc), F32)
    ext_ref[_PAD - 3:_PAD, :] = st_ref[0]
    ext_ref[_PAD:_PAD + t_len, :] = xl
    w = wc_ref[...]
    xc = ext_ref[_PAD - 3:_PAD - 3 + t_len, :] * w[0:1, :]
    xc = xc + ext_ref[_PAD - 2:_PAD - 2 + t_len, :] * w[1:2, :]
    xc = xc + ext_ref[_PAD - 1:_PAD - 1 + t_len, :] * w[2:3, :]
    xc = xc + xl * w[3:4, :]
    xc = xc + bc_ref[...]
    nst_ref[0] = ext_ref[_PAD + t_valid - 3:_PAD + t_valid, :]
    sp = _softplus(-lam_ref[...])
    for blk in range(tc // LANE):
        cs = slice(blk * LANE, (blk + 1) * LANE)
        xb = xc[:, cs]
        xb16 = xb.astype(BF16)
        r = jax.nn.sigmoid(jnp.dot(xb16, wr_ref[blk].astype(BF16), preferred_element_type=F32) + br_ref[:, cs])
        gi = jax.nn.sigmoid(jnp.dot(xb16, wi_ref[blk].astype(BF16), preferred_element_type=F32) + bi_ref[:, cs])
        log_a = (-LRU_C) * r * sp[:, cs]
        a = jnp.exp(log_a)
        a_ref[:, cs] = a
        bx_ref[:, cs] = jnp.sqrt(-jnp.tanh(log_a) * (a * a + 1.0)) * (gi * xb)

    def step(t, h):
        h = a_ref[pl.ds(t, 1), :] * h + bx_ref[pl.ds(t, 1), :]
        ext_ref[pl.ds(t, 1), :] = h
        return h

    lax.fori_loop(0, t_len, step, h0_ref[0], unroll=8)
    hs = ext_ref[0:t_len, :]
    nh_ref[0] = ext_ref[t_valid - 1:t_valid, :]
    o_ref[0] = (hs * jax.nn.gelu(y_ref[0])).astype(o_ref.dtype)


def _rglru(proj, conv_state, h0, w_conv, b_conv, w_r, b_r, w_i, b_i, lam, *, nb, t_len, t_valid, d_half):
    tc = _tile(d_half, 256)
    nc = d_half // tc
    nblk = tc // LANE
    row = lambda a: a.reshape(1, d_half)
    return pl.pallas_call(
        functools.partial(_rglru_kernel, t_len=t_len, t_valid=t_valid),
        grid=(nb, nc),
        in_specs=[pl.BlockSpec((1, t_len, tc), lambda b, j: (b, 0, 3 * nc + j)),
                  pl.BlockSpec((1, t_len, tc), lambda b, j: (b, 0, 4 * nc + j)),
                  pl.BlockSpec((1, 3, tc), lambda b, j: (b, 0, j)),
                  pl.BlockSpec((1, 1, tc), lambda b, j: (b, 0, j)),
                  pl.BlockSpec((4, tc), lambda b, j: (0, j)),
                  pl.BlockSpec((1, tc), lambda b, j: (0, j)),
                  pl.BlockSpec((nblk, LANE, LANE), lambda b, j: (j, 0, 0)),
                  pl.BlockSpec((1, tc), lambda b, j: (0, j)),
                  pl.BlockSpec((nblk, LANE, LANE), lambda b, j: (j, 0, 0)),
                  pl.BlockSpec((1, tc), lambda b, j: (0, j)),
                  pl.BlockSpec((1, tc), lambda b, j: (0, j))],
        out_specs=[pl.BlockSpec((1, t_len, tc), lambda b, j: (b, 0, j)),
                   pl.BlockSpec((1, 3, tc), lambda b, j: (b, 0, j)),
                   pl.BlockSpec((1, 1, tc), lambda b, j: (b, 0, j))],
        out_shape=[jax.ShapeDtypeStruct((nb, t_len, d_half), BF16),
                   jax.ShapeDtypeStruct((nb, 3, d_half), F32),
                   jax.ShapeDtypeStruct((nb, 1, d_half), F32)],
        scratch_shapes=[pltpu.VMEM((_PAD + t_len, tc), F32),
                        pltpu.VMEM((t_len, tc), F32),
                        pltpu.VMEM((t_len, tc), F32)],
        compiler_params=_params(("parallel", "parallel")),
        name="rglru_mixer",
    )(proj, proj, conv_state, h0.reshape(nb, 1, d_half), w_conv, row(b_conv), w_r, row(b_r), w_i, row(b_i),
      row(lam))


_INT_MIN = -2 ** 31


def _sort_key(x):
    bits = pltpu.bitcast(x + 0.0, jnp.int32)
    return bits ^ ((bits >> 31) & jnp.int32(0x7FFFFFFF))


def _strict_tri(n, lower):
    r = lax.broadcasted_iota(jnp.int32, (n, n), 0)
    c = lax.broadcasted_iota(jnp.int32, (n, n), 1)
    return jnp.where((c < r) if lower else (r < c), 1.0, 0.0).astype(BF16)


def _indexer_kernel(iq_ref, ik_ref, w_ref, mask_ref, acc_ref, key_ref, *, tq, s_len, n_heads, k_sel, kc):
    qi = pl.program_id(1)
    acc_ref[...] = jnp.zeros_like(acc_ref)
    n_chunks = s_len // kc

    def head_body(h, carry):
        iq_h = iq_ref[:, pl.ds(pl.multiple_of(h * LANE, LANE), LANE)]
        w_row = w_ref[0, pl.ds(h, 1), :]
        for c in range(n_chunks):
            @pl.when(c * kc < (qi + 1) * tq)
            def _():
                s = lax.dot_general(ik_ref[0, c * kc:(c + 1) * kc, :], iq_h, (((1,), (1,)), ((), ())),
                                    preferred_element_type=F32)
                acc_ref[c * kc:(c + 1) * kc, :] += w_row * jnp.maximum(s, 0.0)
        return carry

    lax.fori_loop(0, n_heads, head_body, 0)

    key_pos = lax.broadcasted_iota(jnp.int32, (s_len, tq), 0)
    q_pos = qi * tq + lax.broadcasted_iota(jnp.int32, (s_len, tq), 1)
    causal = key_pos <= q_pos
    key_ref[...] = _sort_key(jnp.where(causal, acc_ref[...], -jnp.inf))

    def bit_body(i, cu):
        tu = cu | jnp.left_shift(jnp.int32(1), 31 - i)
        ts = tu ^ jnp.int32(_INT_MIN)
        cnt = jnp.sum(jnp.where(key_ref[...] >= ts, 1.0, 0.0), axis=0, keepdims=True)
        return jnp.where(cnt >= float(k_sel), tu, cu)

    cu = lax.fori_loop(0, 32, bit_body, jnp.zeros((1, tq), jnp.int32))
    thr = cu ^ jnp.int32(_INT_MIN)
    need = float(k_sel) - jnp.sum(jnp.where(key_ref[...] > thr, 1.0, 0.0), axis=0, keepdims=True)
    lower = _strict_tri(LANE, True)
    carry = jnp.zeros((1, tq), F32)
    row_in_chunk = lax.broadcasted_iota(jnp.int32, (LANE, tq), 0)
    q_pos_chunk = qi * tq + lax.broadcasted_iota(jnp.int32, (LANE, tq), 1)
    for c in range(s_len // LANE):
        rows = slice(c * LANE, (c + 1) * LANE)
        kc_ = key_ref[rows, :]
        eq = jnp.where(kc_ == thr, 1.0, 0.0)
        before = jnp.dot(lower, eq.astype(BF16), preferred_element_type=F32) + carry
        sel = jnp.logical_or(kc_ > thr, jnp.logical_and(kc_ == thr, before < need))
        sel = jnp.logical_and(sel, row_in_chunk + c * LANE <= q_pos_chunk)
        acc_ref[rows, :] = jnp.where(sel, 0.0, MASK_NEG)
        carry = carry + jnp.sum(eq, axis=0, keepdims=True)
    mask_ref[0] = acc_ref[...].T.astype(mask_ref.dtype)


def _indexer_prompt(iq, ik, w_t, *, nb, s_len, n_heads, k_sel):
    tq = IDX_TQ
    nq = s_len // tq
    kc = _tile(s_len, 512)
    return pl.pallas_call(
        functools.partial(_indexer_kernel, tq=tq, s_len=s_len, n_heads=n_heads, k_sel=k_sel, kc=kc),
        grid=(nb, nq),
        in_specs=[pl.BlockSpec((tq, n_heads * LANE), lambda b, q: (b * nq + q, 0)),
                  pl.BlockSpec((1, s_len, LANE), lambda b, q: (b, 0, 0)),
                  pl.BlockSpec((1, n_heads, tq), lambda b, q: (b, 0, q))],
        out_specs=pl.BlockSpec((1, tq, s_len), lambda b, q: (b, q, 0)),
        out_shape=jax.ShapeDtypeStruct((nb, s_len, s_len), BF16),
        scratch_shapes=[pltpu.VMEM((s_len, tq), F32), pltpu.VMEM((s_len, tq), jnp.int32)],
        compiler_params=_params(("parallel", "parallel")),
        name="indexer_prompt",
    )(iq, ik, w_t)


def _t5_bucket_np(dist):
    max_exact = N_BUCKETS // 2
    d = dist.astype(np.float32)
    ratio = np.log(np.maximum(d, np.float32(1.0)) / np.float32(max_exact)) / np.float32(math.log(MAX_DISTANCE / max_exact))
    large = max_exact + (ratio * np.float32(N_BUCKETS - max_exact)).astype(np.int32)
    large = np.minimum(large, N_BUCKETS - 1)
    return np.where(dist < max_exact, dist, large).astype(np.int32)


def _bias_table_kernel(rb_ref, bk_ref, o_ref):
    h = pl.program_id(0)
    for t in range(bk_ref.shape[0]):
        bk = bk_ref[t]
        acc = jnp.zeros(bk.shape, F32)
        for u in range(N_BUCKETS):
            acc = jnp.where(bk == u, rb_ref[u, h], acc)
        o_ref[0, t] = acc


def _bias_tables(rel_bias, buckets):
    n_tab, r, c = buckets.shape
    n_h = rel_bias.shape[1]
    return pl.pallas_call(
        _bias_table_kernel,
        grid=(n_h,),
        in_specs=[pl.BlockSpec(memory_space=pltpu.SMEM),
                  pl.BlockSpec((n_tab, r, c), lambda h: (0, 0, 0))],
        out_specs=pl.BlockSpec((1, n_tab, r, c), lambda h: (h, 0, 0, 0)),
        out_shape=jax.ShapeDtypeStruct((n_h, n_tab, r, c), F32),
        compiler_params=_params(("arbitrary",)),
        name="bias_tables",
    )(rel_bias, jnp.asarray(buckets))


def _flash_kernel(qi_ref, ki_ref, q_ref, k_ref, v_ref, msk_ref, bias_ref, o_ref, m_ref, l_ref, acc_ref, *,
                  group, scale):
    p = pl.program_id(2)
    qi = qi_ref[p]
    ki = ki_ref[p]

    @pl.when(ki == 0)
    def _():
        m_ref[...] = jnp.full(m_ref.shape, MASK_NEG, F32)
        l_ref[...] = jnp.zeros_like(l_ref)
        acc_ref[...] = jnp.zeros_like(acc_ref)

    kk = k_ref[...]
    vv = v_ref[...]
    mb = msk_ref[0].astype(F32)
    tab = jnp.minimum(qi - ki, 2)
    for g in range(group):
        qg = q_ref[:, g * LANE:(g + 1) * LANE]
        s = lax.dot_general(qg, kk, (((1,), (1,)), ((), ())), preferred_element_type=F32)
        s = (s * scale + bias_ref[g, tab]) + mb
        m_prev = m_ref[g]
        m_new = jnp.maximum(m_prev, jnp.max(s, axis=-1, keepdims=True))
        alpha = jnp.exp(m_prev - m_new)
        pr = jnp.exp(s - m_new)
        l_ref[g] = alpha * l_ref[g] + jnp.sum(pr, axis=-1, keepdims=True)
        acc_ref[g] = alpha * acc_ref[g] + jnp.dot(pr.astype(BF16), vv, preferred_element_type=F32)
        m_ref[g] = m_new

    @pl.when(ki == qi)
    def _():
        for g in range(group):
            o_ref[:, g * LANE:(g + 1) * LANE] = (acc_ref[g] / l_ref[g]).astype(o_ref.dtype)


def _flash_prompt(q, k, v, mask, bias, *, nb, s_len, n_kv, group):
    t = ATT_TILE
    nq = s_len // t
    pairs = [(a, b) for a in range(nq) for b in range(a + 1)]
    qi_tab = jnp.asarray(np.array([a for a, _ in pairs], np.int32))
    ki_tab = jnp.asarray(np.array([b for _, b in pairs], np.int32))
    gw = group * LANE
    grid_spec = pltpu.PrefetchScalarGridSpec(
        num_scalar_prefetch=2,
        grid=(nb, n_kv, len(pairs)),
        in_specs=[pl.BlockSpec((t, gw), lambda b, n, p, qt, kt: (b * nq + qt[p], n)),
                  pl.BlockSpec((t, LANE), lambda b, n, p, qt, kt: (b * nq + kt[p], n)),
                  pl.BlockSpec((t, LANE), lambda b, n, p, qt, kt: (b * nq + kt[p], n)),
                  pl.BlockSpec((1, t, t), lambda b, n, p, qt, kt: (b, qt[p], kt[p])),
                  pl.BlockSpec((group, 3, t, t), lambda b, n, p, qt, kt: (n, 0, 0, 0))],
        out_specs=pl.BlockSpec((t, gw), lambda b, n, p, qt, kt: (b * nq + qt[p], n)),
        scratch_shapes=[pltpu.VMEM((group, t, 1), F32), pltpu.VMEM((group, t, 1), F32),
                        pltpu.VMEM((group, t, LANE), F32)],
    )
    return pl.pallas_call(
        functools.partial(_flash_kernel, group=group, scale=LANE ** -0.5),
        grid_spec=grid_spec,
        out_shape=jax.ShapeDtypeStruct(q.shape, BF16),
        compiler_params=_params(("parallel", "parallel", "arbitrary")),
        name="flash_prompt",
    )(qi_tab, ki_tab, q, k, v, mask, bias)


_PAGES_PER_STEP = 8


def _indexer_sample_kernel(pt_ref, iq_ref, wb_ref, *rest, n_pages, past, k_sel, n_heads):
    page_refs = rest[:_PAGES_PER_STEP]
    new_ref = rest[_PAGES_PER_STEP]
    mask_ref = rest[_PAGES_PER_STEP + 1]
    sc_ref = rest[_PAGES_PER_STEP + 2]
    key_ref = rest[_PAGES_PER_STEP + 3]
    s = pl.program_id(1)
    n_steps = n_pages // _PAGES_PER_STEP

    def page_scores(page):
        z = lax.dot_general(iq_ref[0], page.astype(BF16), (((1,), (1,)), ((), ())),
                            preferred_element_type=F32)
        z = jnp.maximum(z, 0.0) * wb_ref[0]
        return jnp.sum(z.reshape(n_heads, 8, LANE), axis=0)

    @pl.when(s < n_steps)
    def _():
        for u in range(_PAGES_PER_STEP):
            sc_ref[s * _PAGES_PER_STEP + u] = page_scores(page_refs[u][0])

    @pl.when(s == n_steps)
    def _():
        sc_ref[n_pages] = page_scores(new_ref[0])
        shape = (n_pages + 1, 8, LANE)
        pos = lax.broadcasted_iota(jnp.int32, shape, 0) * LANE + lax.broadcasted_iota(jnp.int32, shape, 2)
        q_pos = past + lax.broadcasted_iota(jnp.int32, shape, 1)
        causal = pos <= q_pos
        key_ref[...] = _sort_key(jnp.where(causal, sc_ref[...], -jnp.inf))

        def bit_body(i, cu):
            tu = cu | jnp.left_shift(jnp.int32(1), 31 - i)
            ts = tu ^ jnp.int32(_INT_MIN)
            ind = jnp.where(key_ref[...] >= ts, 1.0, 0.0)
            cnt = jnp.sum(jnp.sum(ind, axis=0), axis=-1, keepdims=True)
            return jnp.where(cnt >= float(k_sel), tu, cu)

        cu = lax.fori_loop(0, 32, bit_body, jnp.zeros((8, 1), jnp.int32))
        thr = cu ^ jnp.int32(_INT_MIN)
        n_gt = jnp.sum(jnp.sum(jnp.where(key_ref[...] > thr[None], 1.0, 0.0), axis=0), axis=-1, keepdims=True)
        need = float(k_sel) - n_gt
        upper = _strict_tri(LANE, False)

        def page_body(pg, carry):
            kp = key_ref[pg]
            eq = jnp.where(kp == thr, 1.0, 0.0)
            before = jnp.dot(eq.astype(BF16), upper, preferred_element_type=F32) + carry
            sel = jnp.logical_or(kp > thr, jnp.logical_and(kp == thr, before < need))
            lane_pos = pg * LANE + lax.broadcasted_iota(jnp.int32, (8, LANE), 1)
            sel = jnp.logical_and(sel, lane_pos <= past + lax.broadcasted_iota(jnp.int32, (8, LANE), 0))
            mask_ref[0, pg] = jnp.where(sel, 0.0, MASK_NEG)
            return carry + jnp.sum(eq, axis=-1, keepdims=True)

        lax.fori_loop(0, n_pages + 1, page_body, jnp.zeros((8, 1), F32))


def _indexer_sample(page_table, iq2, wb, cache_ik, ik_new_page, *, nb, n_pages, past, k_sel, n_heads):
    n_steps = n_pages // _PAGES_PER_STEP
    rows = n_heads * 8

    def page_map(u):
        return lambda b, s, pt: (pt[b, jnp.minimum(s * _PAGES_PER_STEP + u, n_pages - 1)], 0, 0)

    grid_spec = pltpu.PrefetchScalarGridSpec(
        num_scalar_prefetch=1,
        grid=(nb, n_steps + 1),
        in_specs=[pl.BlockSpec((1, rows, LANE), lambda b, s, pt: (b, 0, 0)),
                  pl.BlockSpec((1, rows, LANE), lambda b, s, pt: (b, 0, 0))]
                 + [pl.BlockSpec((1, LANE, LANE), page_map(u)) for u in range(_PAGES_PER_STEP)]
                 + [pl.BlockSpec((1, LANE, LANE), lambda b, s, pt: (b, 0, 0))],
        out_specs=pl.BlockSpec((1, n_pages + 1, 8, LANE), lambda b, s, pt: (b, 0, 0, 0)),
        scratch_shapes=[pltpu.VMEM((n_pages + 1, 8, LANE), F32), pltpu.VMEM((n_pages + 1, 8, LANE), jnp.int32)],
    )
    return pl.pallas_call(
        functools.partial(_indexer_sample_kernel, n_pages=n_pages, past=past, k_sel=k_sel, n_heads=n_heads),
        grid_spec=grid_spec,
        out_shape=jax.ShapeDtypeStruct((nb, n_pages + 1, 8, LANE), F32),
        compiler_params=_params(("parallel", "arbitrary")),
        name="indexer_sample",
    )(page_table, iq2, wb, *([cache_ik] * _PAGES_PER_STEP), ik_new_page)


def _attn_sample_kernel(pt_ref, q_ref, kp_ref, vp_ref, kn_ref, vn_ref, msk_ref, bias_ref, o_ref,
                        m_ref, l_ref, acc_ref, *, n_pages, n_kv, group, scale):
    p = pl.program_id(1)

    @pl.when(p == 0)
    def _():
        m_ref[...] = jnp.full(m_ref.shape, MASK_NEG, F32)
        l_ref[...] = jnp.zeros_like(l_ref)
        acc_ref[...] = jnp.zeros_like(acc_ref)

    mb = msk_ref[0, 0]
    mb = jnp.concatenate([mb] * group, axis=0)
    tab = jnp.where(p == n_pages, 2, jnp.where(p == n_pages - 1, 1, 0))

    def attend(k_ref, v_ref):
        for n in range(n_kv):
            kn = k_ref[0, pl.ds(n, LANE, stride=n_kv), :].astype(BF16)
            vn = v_ref[0, pl.ds(n, LANE, stride=n_kv), :].astype(BF16)
            s = lax.dot_general(q_ref[0, n], kn, (((1,), (1,)), ((), ())), preferred_element_type=F32)
            s = (s * scale + bias_ref[n, tab]) + mb
            m_prev = m_ref[n]
            m_new = jnp.maximum(m_prev, jnp.max(s, axis=-1, keepdims=True))
            alpha = jnp.exp(m_prev - m_new)
            pr = jnp.exp(s - m_new)
            l_ref[n] = alpha * l_ref[n] + jnp.sum(pr, axis=-1, keepdims=True)
            acc_ref[n] = alpha * acc_ref[n] + jnp.dot(pr.astype(BF16), vn, preferred_element_type=F32)
            m_ref[n] = m_new

    @pl.when(p < n_pages)
    def _():
        attend(kp_ref, vp_ref)

    @pl.when(p == n_pages)
    def _():
        attend(kn_ref, vn_ref)
        for n in range(n_kv):
            o_ref[0, n] = (acc_ref[n] / l_ref[n]).astype(o_ref.dtype)


def _attn_sample(page_table, q2, cache_k, cache_v, k_new_page, v_new_page, mask, bias, *, nb, n_pages, n_kv, group):
    rows = group * 8
    prow = LANE * n_kv

    def page_map(b, p, pt):
        return (pt[b, jnp.minimum(p, n_pages - 1)], 0, 0)

    grid_spec = pltpu.PrefetchScalarGridSpec(
        num_scalar_prefetch=1,
        grid=(nb, n_pages + 1),
        in_specs=[pl.BlockSpec((1, n_kv, rows, LANE), lambda b, p, pt: (b, 0, 0, 0)),
                  pl.BlockSpec((1, prow, LANE), page_map),
                  pl.BlockSpec((1, prow, LANE), page_map),
                  pl.BlockSpec((1, prow, LANE), lambda b, p, pt: (b, 0, 0)),
                  pl.BlockSpec((1, prow, LANE), lambda b, p, pt: (b, 0, 0)),
                  pl.BlockSpec((1, 1, 8, LANE), lambda b, p, pt: (b, p, 0, 0)),
                  pl.BlockSpec((n_kv, 3, rows, LANE), lambda b, p, pt: (0, 0, 0, 0))],
        out_specs=pl.BlockSpec((1, n_kv, rows, LANE), lambda b, p, pt: (b, 0, 0, 0)),
        scratch_shapes=[pltpu.VMEM((n_kv, rows, 1), F32), pltpu.VMEM((n_kv, rows, 1), F32),
                        pltpu.VMEM((n_kv, rows, LANE), F32)],
    )
    return pl.pallas_call(
        functools.partial(_attn_sample_kernel, n_pages=n_pages, n_kv=n_kv, group=group, scale=LANE ** -0.5),
        grid_spec=grid_spec,
        out_shape=jax.ShapeDtypeStruct((nb, n_kv, rows, LANE), BF16),
        compiler_params=_params(("parallel", "arbitrary")),
        name="attn_sample",
    )(page_table, q2, cache_k, cache_v, k_new_page, v_new_page, mask, bias)


def _router_kernel(x_ref, g_ref, w_ref, h_ref, r_ref, *, n_experts):
    x = x_ref[...]
    ms = jnp.mean(x * x, axis=-1, keepdims=True)
    h = (x * lax.rsqrt(ms + EPS)) * g_ref[...]
    h_ref[...] = h
    logits = jnp.dot(h, w_ref[...], preferred_element_type=F32, precision=lax.Precision.HIGHEST)
    lane = lax.broadcasted_iota(jnp.int32, logits.shape, 1)
    lg = jnp.where(lane < n_experts, logits, -jnp.inf)
    v1 = jnp.max(lg, axis=-1, keepdims=True)
    i1 = jnp.min(jnp.where(lg == v1, lane, LANE), axis=-1, keepdims=True)
    lg2 = jnp.where(lane == i1, -jnp.inf, lg)
    v2 = jnp.max(lg2, axis=-1, keepdims=True)
    i2 = jnp.min(jnp.where(lg2 == v2, lane, LANE), axis=-1, keepdims=True)
    e2 = jnp.exp(v2 - v1)
    g1 = 1.0 / (1.0 + e2)
    g2 = e2 / (1.0 + e2)
    out = jnp.where(lane == 0, i1.astype(F32), 0.0)
    out = jnp.where(lane == 1, i2.astype(F32), out)
    out = jnp.where(lane == 2, g1, out)
    out = jnp.where(lane == 3, g2, out)
    r_ref[...] = out


def _router(x, g, w_router, *, n_experts):
    m, d = x.shape
    tm = _tile(m, 128, 8)
    w_pad = jnp.zeros((d, LANE), F32).at[:, :n_experts].set(w_router)
    return pl.pallas_call(
        functools.partial(_router_kernel, n_experts=n_experts),
        grid=(m // tm,),
        in_specs=[pl.BlockSpec((tm, d), lambda i: (i, 0)),
                  pl.BlockSpec((1, d), lambda i: (0, 0)),
                  pl.BlockSpec((d, LANE), lambda i: (0, 0))],
        out_specs=[pl.BlockSpec((tm, d), lambda i: (i, 0)), pl.BlockSpec((tm, LANE), lambda i: (i, 0))],
        out_shape=[jax.ShapeDtypeStruct((m, d), F32), jax.ShapeDtypeStruct((m, LANE), F32)],
        compiler_params=_params(("parallel",)),
        name="moe_router",
    )(x, g.reshape(1, d), w_pad)


def _row_copy(src_hbm, row, dst_vmem, slot, sem):
    return pltpu.make_async_copy(src_hbm.at[pl.ds(row, 1)], dst_vmem.at[pl.ds(slot, 1)], sem)


def _gather_rows_kernel(src_ref, h_hbm, o_ref, buf_ref, sem, *, tile):
    base = pl.program_id(0) * tile

    def start(r, c):
        _row_copy(h_hbm, src_ref[base + r], buf_ref, r, sem).start()
        return c

    def wait(r, c):
        _row_copy(h_hbm, src_ref[base + r], buf_ref, r, sem).wait()
        return c

    lax.fori_loop(0, tile, start, 0)
    lax.fori_loop(0, tile, wait, 0)
    o_ref[...] = buf_ref[...].astype(o_ref.dtype)


def _gather_rows(src_row, h, *, n_rows):
    d = h.shape[1]
    tile = MOE_TILE
    grid_spec = pltpu.PrefetchScalarGridSpec(
        num_scalar_prefetch=1,
        grid=(n_rows // tile,),
        in_specs=[pl.BlockSpec(memory_space=pl.ANY)],
        out_specs=pl.BlockSpec((tile, d), lambda i, s: (i, 0)),
        scratch_shapes=[pltpu.VMEM((tile, d), F32), pltpu.SemaphoreType.DMA(())],
    )
    return pl.pallas_call(
        functools.partial(_gather_rows_kernel, tile=tile),
        grid_spec=grid_spec,
        out_shape=jax.ShapeDtypeStruct((n_rows, d), BF16),
        compiler_params=_params(("arbitrary",)),
        name="moe_gather",
    )(src_row, h)


def _moe_gu_kernel(we_ref, wj_ref, wi_ref, oj_ref, wfirst_ref, wreal_ref, x_ref, wg_ref, wu_ref, o_ref, gb_ref,
                   ub_ref):
    w = pl.program_id(0)

    @pl.when(wfirst_ref[w] == 1)
    def _():
        gb_ref[...] = wg_ref[0].astype(BF16)
        ub_ref[...] = wu_ref[0].astype(BF16)

    @pl.when(wreal_ref[w] == 1)
    def _():
        x = x_ref[...]
        g = jnp.dot(x, gb_ref[...], preferred_element_type=F32)
        u = jnp.dot(x, ub_ref[...], preferred_element_type=F32)
        o_ref[...] = ((g * jax.nn.sigmoid(g)) * u).astype(o_ref.dtype)

    @pl.when(wreal_ref[w] == 0)
    def _():
        o_ref[...] = jnp.zeros_like(o_ref)


def _moe_gate_up(work, xs, w_gate, w_up, *, tn):
    we, wj, wi, oj, wfirst, wreal = work
    p_rows, d = xs.shape
    f = w_gate.shape[2]
    grid_spec = pltpu.PrefetchScalarGridSpec(
        num_scalar_prefetch=6,
        grid=(we.shape[0],),
        in_specs=[pl.BlockSpec((MOE_TILE, d), lambda w, e, j, i, o, a, b: (i[w], 0)),
                  pl.BlockSpec((1, d, tn), lambda w, e, j, i, o, a, b: (e[w], 0, j[w])),
                  pl.BlockSpec((1, d, tn), lambda w, e, j, i, o, a, b: (e[w], 0, j[w]))],
        out_specs=pl.BlockSpec((MOE_TILE, tn), lambda w, e, j, i, o, a, b: (i[w], o[w])),
        scratch_shapes=[pltpu.VMEM((d, tn), BF16), pltpu.VMEM((d, tn), BF16)],
    )
    return pl.pallas_call(
        _moe_gu_kernel,
        grid_spec=grid_spec,
        out_shape=jax.ShapeDtypeStruct((p_rows, f), BF16),
        compiler_params=_params(("arbitrary",)),
        name="moe_gate_up",
    )(we, wj, wi, oj, wfirst, wreal, xs, w_gate, w_up)


def _moe_down_kernel(te_ref, tv_ref, x_ref, w_ref, o_ref, *, n_k):
    t = pl.program_id(0)
    kk = pl.program_id(1)

    @pl.when(kk == 0)
    def _():
        o_ref[...] = jnp.zeros_like(o_ref)

    @pl.when(tv_ref[t] == 1)
    def _():
        o_ref[...] += jnp.dot(x_ref[...], w_ref[0].astype(BF16), preferred_element_type=F32)


def _moe_down(tile_expert, tile_valid, hmid, w_down, *, tk):
    p_rows, f = hmid.shape
    d = w_down.shape[2]
    n_k = f // tk
    tm = MOE_DOWN_TILE

    def w_map(t, kk, te, tv):
        return (te[t], jnp.where(tv[t] == 1, kk, n_k - 1), 0)

    grid_spec = pltpu.PrefetchScalarGridSpec(
        num_scalar_prefetch=2,
        grid=(p_rows // tm, n_k),
        in_specs=[pl.BlockSpec((tm, tk), lambda t, kk, te, tv: (t, kk)),
                  pl.BlockSpec((1, tk, d), w_map)],
        out_specs=pl.BlockSpec((tm, d), lambda t, kk, te, tv: (t, 0)),
    )
    return pl.pallas_call(
        functools.partial(_moe_down_kernel, n_k=n_k),
        grid_spec=grid_spec,
        out_shape=jax.ShapeDtypeStruct((p_rows, d), F32),
        compiler_params=_params(("arbitrary", "arbitrary")),
        name="moe_down",
    )(tile_expert, tile_valid, hmid, w_down)


def _moe_combine_kernel(pos_ref, ys_hbm, x_ref, g_ref, o_ref, buf_ref, sem, *, tile):
    base = pl.program_id(0) * tile

    def start(r, c):
        _row_copy(ys_hbm, pos_ref[2 * (base + r)], buf_ref.at[0], r, sem).start()
        _row_copy(ys_hbm, pos_ref[2 * (base + r) + 1], buf_ref.at[1], r, sem).start()
        return c

    def wait(r, c):
        _row_copy(ys_hbm, pos_ref[2 * (base + r)], buf_ref.at[0], r, sem).wait()
        _row_copy(ys_hbm, pos_ref[2 * (base + r) + 1], buf_ref.at[1], r, sem).wait()
        return c

    lax.fori_loop(0, tile, start, 0)
    lax.fori_loop(0, tile, wait, 0)
    g = g_ref[...]
    moe = g[:, 2:3] * buf_ref[0] + g[:, 3:4] * buf_ref[1]
    o_ref[...] = x_ref[...] + moe


def _moe_combine(pos, ys, x, route):
    m, d = x.shape
    tile = _tile(m, 128, 8)
    grid_spec = pltpu.PrefetchScalarGridSpec(
        num_scalar_prefetch=1,
        grid=(m // tile,),
        in_specs=[pl.BlockSpec(memory_space=pl.ANY),
                  pl.BlockSpec((tile, d), lambda i, s: (i, 0)),
                  pl.BlockSpec((tile, LANE), lambda i, s: (i, 0))],
        out_specs=pl.BlockSpec((tile, d), lambda i, s: (i, 0)),
        scratch_shapes=[pltpu.VMEM((2, tile, d), F32), pltpu.SemaphoreType.DMA(())],
    )
    return pl.pallas_call(
        functools.partial(_moe_combine_kernel, tile=tile),
        grid_spec=grid_spec,
        out_shape=jax.ShapeDtypeStruct((m, d), F32),
        compiler_params=_params(("arbitrary",)),
        name="moe_combine",
    )(pos, ys, x, route)


def _moe_plan(experts, n_experts, n_f_tiles):
    n_tok = experts.shape[0]
    n_asg = 2 * n_tok
    e_flat = experts.reshape(-1)
    onehot = (e_flat[:, None] == jnp.arange(n_experts, dtype=jnp.int32)[None, :]).astype(jnp.int32)
    counts = jnp.sum(onehot, axis=0)
    rank = jnp.sum((jnp.cumsum(onehot, axis=0) - 1) * onehot, axis=1)
    padded = ((counts + MOE_DOWN_TILE - 1) // MOE_DOWN_TILE) * MOE_DOWN_TILE
    gend = jnp.cumsum(padded)
    gstart = gend - padded
    p_rows = ((n_asg + n_experts * (MOE_DOWN_TILE - 1)) // MOE_DOWN_TILE) * MOE_DOWN_TILE
    pos = (gstart[e_flat] + rank).astype(jnp.int32)
    src_row = jnp.zeros((p_rows,), jnp.int32).at[pos].set(jnp.arange(n_asg, dtype=jnp.int32) // 2)
    n_dt = p_rows // MOE_DOWN_TILE
    dt_start = jnp.arange(n_dt, dtype=jnp.int32) * MOE_DOWN_TILE
    dt_e = jnp.minimum(jnp.searchsorted(gend, dt_start, side="right"), n_experts - 1).astype(jnp.int32)
    dt_valid = (dt_start < gstart[dt_e] + counts[dt_e]).astype(jnp.int32)
    last_valid_e = jnp.max(jnp.where(dt_valid == 1, dt_e, 0))
    dt_e = jnp.where(dt_valid == 1, dt_e, last_valid_e).astype(jnp.int32)
    nt = padded // MOE_TILE
    t_start = gstart // MOE_TILE
    items = nt * n_f_tiles
    w_end = jnp.cumsum(items)
    w_start = w_end - items
    total = w_end[-1]
    w_max = n_f_tiles * (p_rows // MOE_TILE)
    widx = jnp.arange(w_max, dtype=jnp.int32)
    wreal = widx < total
    wq = jnp.minimum(widx, total - 1)
    we = jnp.minimum(jnp.searchsorted(w_end, wq, side="right"), n_experts - 1).astype(jnp.int32)
    local = wq - w_start[we]
    nt_e = jnp.maximum(nt[we], 1)
    wj = (local // nt_e).astype(jnp.int32)
    li = local % nt_e
    z = widx - total
    wi = jnp.where(wreal, t_start[we] + li, gend[-1] // MOE_TILE + z // n_f_tiles).astype(jnp.int32)
    oj = jnp.where(wreal, wj, z % n_f_tiles).astype(jnp.int32)
    wfirst = jnp.logical_and(li == 0, wreal).astype(jnp.int32)
    work = (we, wj, wi, oj, wfirst, wreal.astype(jnp.int32))
    return pos, src_row, p_rows, dt_e, dt_valid, work


def _moe(x_all, g_ffn, w_router, w_gate, w_up, w_down, cfg):
    n_experts = cfg.n_experts
    h, route = _router(x_all, g_ffn, w_router, n_experts=n_experts)
    experts = route[:, 0:2].astype(jnp.int32)
    tn = _tile(cfg.d_ff_expert, 512)
    pos, src_row, p_rows, dt_e, dt_valid, work = _moe_plan(experts, n_experts, cfg.d_ff_expert // tn)
    xs = _gather_rows(src_row, h, n_rows=p_rows)
    hmid = _moe_gate_up(work, xs, w_gate, w_up, tn=tn)
    ys = _moe_down(dt_e, dt_valid, hmid, w_down, tk=_tile(cfg.d_ff_expert, 512))
    return _moe_combine(pos, ys, x_all, route)


def _layer0(x, conv_state, lru_state, lru_h, p, cfg, *, nb, t_len, t_valid):
    d = cfg.d_model
    dh = cfg.d_half
    h = _rmsnorm(x, p["g_mix0"], BF16)
    (proj,) = _mm([h], p["w_in_ab"], tm_pref=1024)
    proj3 = proj.reshape(nb, t_len, 5 * dh)
    out_a, new_conv = _short_conv(proj3, conv_state, p["w_conv_a"], nb=nb, t_len=t_len, t_valid=t_valid, d_half=dh)
    out_l, new_lbuf, new_h = _rglru(proj3, lru_state, lru_h, p["w_conv_lru"], p["b_conv_lru"], p["w_gate_r"],
                                    p["b_gate_r"], p["w_gate_i"], p["b_gate_i"], p["lru_lambda"],
                                    nb=nb, t_len=t_len, t_valid=t_valid, d_half=dh)
    (x,) = _mm([out_a.reshape(nb * t_len, dh), out_l.reshape(nb * t_len, dh)], p["w_out_ab"], res=x, tm_pref=1024)
    h2 = _rmsnorm(x, p["g_ffn0"], BF16)
    hmid = _ffn_gate_up(h2, p["w_ffn_gate"], p["w_ffn_up"])
    (x,) = _mm([hmid], p["w_ffn_down"], res=x, tm_pref=512, tn_pref=256)
    return x, new_conv, new_lbuf, new_h.reshape(nb, dh)


def _attn_proj(x, p, cfg):
    h = _rmsnorm(x, p["g_mix1"], BF16)
    w = p["w_in_attn"]
    qd, kvd, iqd = cfg.q_dim, cfg.kv_dim, cfg.idx_q_dim
    (q,) = _mm([h], w, col_start=0, n_cols=qd, norm_g=p["q_norm"], out_dtypes=(BF16,), tm_pref=1024)
    k32, k16 = _mm([h], w, col_start=qd, n_cols=kvd, norm_g=p["k_norm"], out_dtypes=(F32, BF16), tm_pref=1024)
    v32, v16 = _mm([h], w, col_start=qd + kvd, n_cols=kvd, out_dtypes=(F32, BF16), tm_pref=1024)
    (iq,) = _mm([h], w, col_start=qd + 2 * kvd, n_cols=iqd, out_dtypes=(BF16,), tm_pref=1024)
    (tail,) = _mm([h], p["w_in_attn_tail"], out_dtypes=(F32,), tm_pref=1024)
    ik32 = tail[:, :LANE]
    iw = tail[:, LANE:] * (cfg.idx_q_dim ** -0.5)
    return q, k32, k16, v32, v16, iq, ik32, iw


def _prompt_bias_buckets():
    t = ATT_TILE
    i = np.arange(t, dtype=np.int64)[:, None]
    j = np.arange(t, dtype=np.int64)[None, :]
    d0 = np.maximum(i - j, 0)
    d1 = t + i - j
    d2 = np.full((t, t), 2 * t, np.int64)
    return np.stack([_t5_bucket_np(d0), _t5_bucket_np(d1), _t5_bucket_np(d2)])


def _sample_bias_buckets():
    t = np.arange(8, dtype=np.int64)[:, None]
    o = np.arange(LANE, dtype=np.int64)[None, :]
    far = np.full((8, LANE), 2 * LANE, np.int64)
    last = LANE + t - o
    new = np.maximum(t - o, 0)
    return np.stack([_t5_bucket_np(far), _t5_bucket_np(last), _t5_bucket_np(new)])


def _layer1_prompt(x, p, cfg):
    nb, s_len = cfg.batch, cfg.seq
    q, k32, k16, v32, v16, iq, ik32, iw = _attn_proj(x, p, cfg)
    k_sel = min(cfg.topk_max, s_len // 4)
    w_t = iw.reshape(nb, s_len, cfg.idx_heads).transpose(0, 2, 1)
    mask = _indexer_prompt(iq, ik32.astype(BF16).reshape(nb, s_len, LANE), w_t, nb=nb, s_len=s_len,
                           n_heads=cfg.idx_heads, k_sel=k_sel)
    bias = _bias_tables(p["rel_bias"], _prompt_bias_buckets())
    att = _flash_prompt(q, k16, v16, mask, bias, nb=nb, s_len=s_len, n_kv=cfg.n_kv, group=cfg.group)
    (x,) = _mm([att], p["w_out_attn"], res=x, tm_pref=1024)
    return x, k32, v32, ik32


def _layer1_sample(x, paged, p, cfg):
    nb = cfg.dec_batch
    cache_k, cache_v, cache_ik, page_table = paged
    q, k32, k16, v32, v16, iq, ik32, iw = _attn_proj(x, p, cfg)
    n_pages = cfg.n_pages
    past = n_pages * LANE
    k_sel = min(cfg.topk_max, (past + cfg.dec_seq) // 4)
    ih = cfg.idx_heads
    iq2 = iq.reshape(nb, 8, ih, LANE).transpose(0, 2, 1, 3).reshape(nb, ih * 8, LANE)
    wb = jnp.broadcast_to(iw.reshape(nb, 8, ih).transpose(0, 2, 1).reshape(nb, ih * 8, 1), (nb, ih * 8, LANE))
    ik_new = jnp.zeros((nb, LANE, LANE), F32).at[:, :8].set(ik32.reshape(nb, 8, LANE))
    mask = _indexer_sample(page_table, iq2, wb, cache_ik, ik_new, nb=nb, n_pages=n_pages, past=past, k_sel=k_sel,
                           n_heads=ih)
    bias = _bias_tables(p["rel_bias"], _sample_bias_buckets())
    bias = bias.reshape(cfg.n_kv, cfg.group, 3, 8, LANE).transpose(0, 2, 1, 3, 4).reshape(
        cfg.n_kv, 3, cfg.group * 8, LANE)
    q2 = q.reshape(nb, 8, cfg.n_kv, cfg.group, LANE).transpose(0, 2, 3, 1, 4).reshape(
        nb, cfg.n_kv, cfg.group * 8, LANE)
    prow = LANE * cfg.n_kv
    k_new = jnp.zeros((nb, prow, LANE), F32).at[:, :8 * cfg.n_kv].set(k32.reshape(nb, 8 * cfg.n_kv, LANE))
    v_new = jnp.zeros((nb, prow, LANE), F32).at[:, :8 * cfg.n_kv].set(v32.reshape(nb, 8 * cfg.n_kv, LANE))
    att = _attn_sample(page_table, q2, cache_k, cache_v, k_new, v_new, mask, bias, nb=nb, n_pages=n_pages,
                       n_kv=cfg.n_kv, group=cfg.group)
    att = att.reshape(nb, cfg.n_kv, cfg.group, 8, LANE).transpose(0, 3, 1, 2, 4).reshape(nb * 8, cfg.q_dim)
    (x,) = _mm([att], p["w_out_attn"], res=x, tm_pref=1024)
    return x, k32, v32, ik32


def _forward(cfg, x_prompt, x_sample, state_conv_a, state_lru_conv, state_lru_h, cache_k, cache_v, cache_idx_k,
             page_table, g_mix, g_ffn, w_in_ab, w_conv_a, w_conv_lru, b_conv_lru, w_gate_r, b_gate_r, w_gate_i,
             b_gate_i, lru_lambda, w_out_ab, w_ffn_gate, w_ffn_up, w_ffn_down, w_in_attn, q_norm, k_norm, rel_bias,
             w_out_attn, w_router, w_exp_gate, w_exp_up, w_exp_down):
    d = cfg.d_model
    dh = cfg.d_half
    nbp, s_len = cfg.batch, cfg.seq
    nbs, t_dec = cfg.dec_batch, cfg.dec_seq
    tail_start = cfg.q_dim + 2 * cfg.kv_dim + cfg.idx_q_dim
    w_attn16 = w_in_attn[0].astype(BF16)
    p = {
        "g_mix0": g_mix[0], "g_mix1": g_mix[1], "g_ffn0": g_ffn[0], "g_ffn1": g_ffn[1],
        "w_in_ab": w_in_ab[0].astype(BF16), "w_conv_a": w_conv_a[0], "w_conv_lru": w_conv_lru[0],
        "b_conv_lru": b_conv_lru[0], "w_gate_r": w_gate_r[0], "b_gate_r": b_gate_r[0], "w_gate_i": w_gate_i[0],
        "b_gate_i": b_gate_i[0], "lru_lambda": lru_lambda[0], "w_out_ab": w_out_ab[0].astype(BF16),
        "w_ffn_gate": w_ffn_gate[0].astype(BF16), "w_ffn_up": w_ffn_up[0].astype(BF16),
        "w_ffn_down": w_ffn_down[0].astype(BF16), "w_in_attn": w_attn16,
        "w_in_attn_tail": w_attn16[:, tail_start:], "q_norm": q_norm[0], "k_norm": k_norm[0],
        "rel_bias": rel_bias, "w_out_attn": w_out_attn[0].astype(BF16),
    }
    xp = x_prompt.reshape(nbp * s_len, d)
    xp, p_conv, p_lbuf, p_h = _layer0(xp, jnp.zeros((nbp, 2, dh), F32), jnp.zeros((nbp, 3, dh), F32),
                                      jnp.zeros((nbp, dh), F32), p, cfg, nb=nbp, t_len=s_len, t_valid=s_len)
    xp, p_k, p_v, p_ik = _layer1_prompt(xp, p, cfg)
    xs = jnp.zeros((nbs, 8, d), F32).at[:, :t_dec].set(x_sample).reshape(nbs * 8, d)
    xs, s_conv, s_lbuf, s_h = _layer0(xs, state_conv_a[0], state_lru_conv[0], state_lru_h[0], p, cfg,
                                      nb=nbs, t_len=8, t_valid=t_dec)
    n_pool = cache_k.shape[1]
    paged = (cache_k[0].reshape(n_pool, LANE * cfg.n_kv, LANE), cache_v[0].reshape(n_pool, LANE * cfg.n_kv, LANE),
             cache_idx_k[0], page_table)
    xs, s_k, s_v, s_ik = _layer1_sample(xs, paged, p, cfg)
    n_p = nbp * s_len
    n_s = nbs * 8
    x_all = jnp.concatenate([xp, xs], axis=0)
    n_all = n_p + n_s
    n_pad = (-n_all) % LANE
    if n_pad:
        x_all = jnp.concatenate([x_all, jnp.zeros((n_pad, d), F32)], axis=0)
    y_all = _moe(x_all, g_ffn[1], w_router[0], w_exp_gate[0], w_exp_up[0], w_exp_down[0], cfg)
    y_prompt = y_all[:n_p].reshape(nbp, s_len, d)
    y_sample = y_all[n_p:n_p + n_s].reshape(nbs, 8, d)[:, :t_dec]

    def rows(a, nb, t, *shape):
        return a.reshape(nb, t, *shape)

    kv_shape = (cfg.n_kv, LANE)
    s_k = rows(s_k, nbs, 8, *kv_shape)[:, :t_dec]
    s_v = rows(s_v, nbs, 8, *kv_shape)[:, :t_dec]
    s_ik = rows(s_ik, nbs, 8, LANE)[:, :t_dec]
    return (y_prompt, y_sample,
            p_conv[None], p_lbuf[None], p_h[None],
            rows(p_k, nbp, s_len, *kv_shape)[None], rows(p_v, nbp, s_len, *kv_shape)[None],
            rows(p_ik, nbp, s_len, LANE)[None],
            s_conv[None], s_lbuf[None], s_h[None], s_k[None], s_v[None], s_ik[None])


def kernel(x_prompt, x_sample, state_conv_a, state_lru_conv, state_lru_h, cache_k, cache_v, cache_idx_k, page_table, g_mix, g_ffn, w_in_ab, w_conv_a, w_conv_lru, b_conv_lru, w_gate_r, b_gate_r, w_gate_i, b_gate_i, lru_lambda, w_out_ab, w_ffn_gate, w_ffn_up, w_ffn_down, w_in_attn, q_norm, k_norm, rel_bias, w_out_attn, w_router, w_exp_gate, w_exp_up, w_exp_down):
    return _forward(_CFG, x_prompt, x_sample, state_conv_a, state_lru_conv, state_lru_h, cache_k, cache_v,
                    cache_idx_k, page_table, g_mix, g_ffn, w_in_ab, w_conv_a, w_conv_lru, b_conv_lru, w_gate_r,
                    b_gate_r, w_gate_i, b_gate_i, lru_lambda, w_out_ab, w_ffn_gate, w_ffn_up, w_ffn_down, w_in_attn,
                    q_norm, k_norm, rel_bias, w_out_attn, w_router, w_exp_gate, w_exp_up, w_exp_down)
```
